```python
import jax, jax.numpy as jnp
from jax import lax
import numpy as np

D_MODEL = 1024
BATCH = 8
SEQ = 2048
DEPTH = 4
DEC_BATCH = 128
DEC_SEQ = 8
PAST_LEN = 16384
PAGE_SIZE = 128

N_META = 16
CHUNK = 64
CONV_W = 4
GDN_HEADS = 4
GDN_DK = 128
GDN_DV = 128
SSD_HEADS = 8
SSD_P = 64
SSD_N = 128
SSD_GROUPS = 2
GDN_QK = GDN_HEADS * GDN_DK
GDN_V = GDN_HEADS * GDN_DV
SSD_INNER = SSD_HEADS * SSD_P
SSD_BC = SSD_GROUPS * SSD_N
MIX_WIDTH = GDN_V + SSD_INNER
GDN_CONV_DIM = 2 * GDN_QK + GDN_V
SSD_CONV_DIM = SSD_INNER + 2 * SSD_BC
IN_DIM = GDN_CONV_DIM + GDN_V + 2 * GDN_HEADS + SSD_INNER + SSD_CONV_DIM + SSD_HEADS
N_EGROUPS = 4
E_PER_GROUP = 8
N_EXPERTS = N_EGROUPS * E_PER_GROUP
TOP_K = 2
D_EXPERT = D_MODEL // 2
MOE_BLOCK = 128
DEEPNORM_ALPHA = (2 * DEPTH) ** 0.25
DEEPNORM_BETA = (8 * DEPTH) ** -0.25
EPS = 1e-6

kernel_name = 'hymba_gdn_ssd_hmoe_step'


def _layer_norm(x, g, b):
    xf = x.astype(jnp.float32)
    mu = jnp.mean(xf, axis=-1, keepdims=True)
    var = jnp.mean(jnp.square(xf - mu), axis=-1, keepdims=True)
    return ((xf - mu) * lax.rsqrt(var + EPS) * g.astype(jnp.float32) + b.astype(jnp.float32)).astype(x.dtype)


def _rms(x):
    return x * lax.rsqrt(jnp.mean(jnp.square(x), axis=-1, keepdims=True) + EPS)


def _l2norm(x):
    return x * lax.rsqrt(jnp.sum(jnp.square(x), axis=-1, keepdims=True) + EPS)


def _causal_conv(x, buf, w, b):
    L = x.shape[1]
    xp = jnp.concatenate([buf.astype(x.dtype), x], axis=1)
    y = sum(xp[:, i:i + L] * w[i] for i in range(CONV_W))
    if b is not None:
        y = y + b
    return jax.nn.silu(y), xp[:, L:]


def _to_chunks(a, chunk):
    b_, h_, L = a.shape[:3]
    nc = -(-L // chunk)
    a = jnp.pad(a, [(0, 0), (0, 0), (0, nc * chunk - L)] + [(0, 0)] * (a.ndim - 3))
    a = a.reshape((b_, h_, nc, chunk) + a.shape[3:])
    return jnp.moveaxis(a, 2, 0)


def _from_chunks(a, L):
    a = jnp.moveaxis(a, 0, 2)
    a = a.reshape(a.shape[:2] + (-1,) + a.shape[4:])
    return a[:, :, :L]


def _decay_matrices(g):
    gam = jnp.cumsum(g, axis=-1)
    c = g.shape[-1]
    causal = jnp.tril(jnp.ones((c, c), dtype=bool))
    diff = gam[..., :, None] - gam[..., None, :]
    return gam, jnp.exp(jnp.where(causal, diff, -jnp.inf))


def _gdn_segment(q, k, v, g, beta, s0, chunk):
    L = q.shape[2]
    qc, kc, vc, gc, bc = [_to_chunks(a, chunk) for a in (q, k, v, g, beta)]
    gam, dec = _decay_matrices(gc)
    eye = jnp.eye(chunk, dtype=qc.dtype)
    kk = jnp.einsum('nbhcd,nbhed->nbhce', kc, kc)
    m = eye + bc[..., :, None] * kk * dec * (1.0 - eye)
    u_t = lax.linalg.triangular_solve(m, bc[..., None] * vc, left_side=True, lower=True)
    w = lax.linalg.triangular_solve(m, (bc * jnp.exp(gam))[..., None] * kc, left_side=True, lower=True)
    qk = jnp.einsum('nbhcd,nbhed->nbhce', qc, kc) * dec
    q_dec = qc * jnp.exp(gam)[..., None]
    k_dec = kc * jnp.exp(gam[..., -1:] - gam)[..., None]
    g_end = jnp.exp(gam[..., -1])

    def step(s, inp):
        u_c, w_c, qk_c, q_c, k_c, ge = inp
        u = u_c - jnp.einsum('bhcd,bhde->bhce', w_c, s)
        o = jnp.einsum('bhcd,bhde->bhce', q_c, s) + jnp.einsum('bhce,bhef->bhcf', qk_c, u)
        s = ge[..., None, None] * s + jnp.einsum('bhcd,bhce->bhde', k_c, u)
        return s, o

    s, o = lax.scan(step, s0, (u_t, w, qk, q_dec, k_dec, g_end))
    return _from_chunks(o, L), s


def _ssd_segment(x, dt, g, bm, cm, h0, chunk):
    L = x.shape[2]
    xc, dtc, gc, bc, cc = [_to_chunks(a, chunk) for a in (x, dt, g, bm, cm)]
    gam, dec = _decay_matrices(gc)
    xdt = xc * dtc[..., None]
    cb = jnp.einsum('nbhts,nbhjs->nbhtj', cc, bc) * dec
    c_dec = cc * jnp.exp(gam)[..., None]
    b_dec = bc * jnp.exp(gam[..., -1:] - gam)[..., None]
    g_end = jnp.exp(gam[..., -1])

    def step(hs, inp):
        xdt_c, cb_c, c_c, b_c, ge = inp
        y = jnp.einsum('bhts,bhps->bhtp', c_c, hs) + jnp.einsum('bhtj,bhjp->bhtp', cb_c, xdt_c)
        hs = ge[..., None, None] * hs + jnp.einsum('bhjp,bhjs->bhps', xdt_c, b_c)
        return hs, y

    hs, y = lax.scan(step, h0, (xdt, cb, c_dec, b_dec, g_end))
    return _from_chunks(y, L), hs


def _run_segments(segment_fn, seqs, state, segments):
    outs, start = [], 0
    for length, chunk in segments:
        part = [s[:, :, start:start + length] for s in seqs]
        out, state = segment_fn(*part, state, chunk)
        outs.append(out)
        start += length
    return jnp.concatenate(outs, axis=2), state


def _mixer(h, s_a, conv_a, s_b, conv_b, segments, w_in, gdn_conv_w, gdn_a_log, gdn_dt_bias, gdn_norm_w,
           ssd_conv_w, ssd_conv_b, ssd_a_log, ssd_dt_bias, ssd_d, ssd_norm_w, w_out):
    f32 = jnp.float32
    bsz, L, _ = h.shape
    proj = h @ w_in
    cuts = np.cumsum([GDN_CONV_DIM, GDN_V, GDN_HEADS, GDN_HEADS, SSD_INNER, SSD_CONV_DIM]).tolist()
    qkv, z_a, b_a, a_a, z_b, xbc, dt_b = jnp.split(proj, cuts, axis=-1)

    def heads(t, d):
        return t.reshape(bsz, L, -1, d).transpose(0, 2, 1, 3)

    qkv, new_conv_a = _causal_conv(qkv, conv_a, gdn_conv_w, None)
    q, k, v = jnp.split(qkv.astype(f32), [GDN_QK, 2 * GDN_QK], axis=-1)
    q = _l2norm(heads(q, GDN_DK)) * (GDN_DK ** -0.5)
    k = _l2norm(heads(k, GDN_DK))
    v = heads(v, GDN_DV)
    beta = jax.nn.sigmoid(b_a.astype(f32)).transpose(0, 2, 1)
    g_a = (-jnp.exp(gdn_a_log.astype(f32))[:, None]
           * jax.nn.softplus(a_a.astype(f32) + gdn_dt_bias.astype(f32)).transpose(0, 2, 1))
    o_a, s_a = _run_segments(_gdn_segment, (q, k, v, g_a, beta), s_a.astype(f32), segments)
    o_a = o_a.transpose(0, 2, 1, 3)
    o_a = _rms(o_a) * gdn_norm_w.astype(f32) * jax.nn.silu(z_a.astype(f32)).reshape(bsz, L, GDN_HEADS, GDN_DV)
    o_a = o_a.reshape(bsz, L, GDN_V)

    xbc, new_conv_b = _causal_conv(xbc, conv_b, ssd_conv_w, ssd_conv_b)
    xs, bm, cm = jnp.split(xbc.astype(f32), [SSD_INNER, SSD_INNER + SSD_BC], axis=-1)
    xs = heads(xs, SSD_P)
    rep = SSD_HEADS // SSD_GROUPS
    bm = jnp.repeat(heads(bm, SSD_N), rep, axis=1)
    cm = jnp.repeat(heads(cm, SSD_N), rep, axis=1)
    dt = jax.nn.softplus(dt_b.astype(f32) + ssd_dt_bias.astype(f32)).transpose(0, 2, 1)
    g_b = -jnp.exp(ssd_a_log.astype(f32))[:, None] * dt
    y_b, s_b = _run_segments(_ssd_segment, (xs, dt, g_b, bm, cm), s_b.astype(f32), segments)
    y_b = y_b + ssd_d.astype(f32)[:, None, None] * xs
    y_b = y_b.transpose(0, 2, 1, 3).reshape(bsz, L, SSD_INNER) * jax.nn.silu(z_b.astype(f32))
    y_b = _rms(y_b.reshape(bsz, L, SSD_GROUPS, -1)).reshape(bsz, L, SSD_INNER) * ssd_norm_w.astype(f32)

    mix = jnp.concatenate([o_a, y_b], axis=-1).astype(h.dtype) @ w_out
    return mix.astype(h.dtype), s_a, new_conv_a, s_b, new_conv_b


def _moe(h, router_g, router_e, w1, w3, w2):
    T = h.shape[0]
    hf = h.astype(jnp.float32)
    g_prob = jax.nn.softmax(hf @ router_g.astype(jnp.float32), axis=-1)
    g_val, g_idx = lax.top_k(g_prob, 1)
    e_logits = (hf @ router_e.astype(jnp.float32)).reshape(T, N_EGROUPS, E_PER_GROUP)
    e_logits = jnp.take_along_axis(e_logits, g_idx[:, :, None], axis=1)[:, 0]
    e_val, e_idx = lax.top_k(jax.nn.softmax(e_logits, axis=-1), TOP_K)
    gate = g_val * e_val / jnp.sum(e_val, axis=-1, keepdims=True)
    expert = g_idx * E_PER_GROUP + e_idx

    A = T * TOP_K
    e_flat = expert.reshape(-1)
    tok_flat = jnp.repeat(jnp.arange(T, dtype=jnp.int32), TOP_K)
    w_flat = gate.reshape(-1)
    order = jnp.argsort(e_flat)
    e_sorted = e_flat[order]
    counts = jnp.bincount(e_flat, length=N_EXPERTS)
    padded = ((counts + MOE_BLOCK - 1) // MOE_BLOCK) * MOE_BLOCK
    pad_end = jnp.cumsum(padded)
    pad_start = pad_end - padded
    start = jnp.cumsum(counts) - counts
    dest = pad_start[e_sorted] + (jnp.arange(A) - start[e_sorted])
    nb = -(-A // MOE_BLOCK) + N_EXPERTS
    slot_tok = jnp.zeros((nb * MOE_BLOCK,), jnp.int32).at[dest].set(tok_flat[order])
    slot_w = jnp.zeros((nb * MOE_BLOCK,), jnp.float32).at[dest].set(w_flat[order])
    block_exp = jnp.minimum(jnp.searchsorted(pad_end, jnp.arange(nb) * MOE_BLOCK, side='right'), N_EXPERTS - 1)

    def block(args):
        tok, e = args
        xb = h[tok]
        return (jax.nn.silu(xb @ w1[e]) * (xb @ w3[e])) @ w2[e]

    out = lax.map(block, (slot_tok.reshape(nb, MOE_BLOCK), block_exp))
    y = jnp.zeros((T, D_MODEL), jnp.float32).at[slot_tok].add(
        out.reshape(-1, D_MODEL).astype(jnp.float32) * slot_w[:, None])
    return y.astype(h.dtype)


def _trunk(x, s_a, conv_a, s_b, conv_b, segments, params):
    (w_in, gdn_conv_w, gdn_a_log, gdn_dt_bias, gdn_norm_w, ssd_conv_w, ssd_conv_b, ssd_a_log,
     ssd_dt_bias, ssd_d, ssd_norm_w, w_out, ln1_g, ln1_b, router_g, router_e, w1, w3, w2, ln2_g, ln2_b) = params
    bsz, L, _ = x.shape
    out_sa, out_ca, out_sb, out_cb = [], [], [], []
    for l in range(DEPTH):
        mix, sa, ca, sb, cb = _mixer(x, s_a[l], conv_a[l], s_b[l], conv_b[l], segments, w_in[l], gdn_conv_w[l],
                                     gdn_a_log[l], gdn_dt_bias[l], gdn_norm_w[l], ssd_conv_w[l], ssd_conv_b[l],
                                     ssd_a_log[l], ssd_dt_bias[l], ssd_d[l], ssd_norm_w[l], w_out[l])
        out_sa.append(sa.astype(s_a.dtype))
        out_ca.append(ca.astype(conv_a.dtype))
        out_sb.append(sb.astype(s_b.dtype))
        out_cb.append(cb.astype(conv_b.dtype))
        x = _layer_norm(DEEPNORM_ALPHA * x + mix, ln1_g[l], ln1_b[l])
        ffn = _moe(x.reshape(bsz * L, D_MODEL), router_g[l], router_e[l], w1[l], w3[l], w2[l]).reshape(x.shape)
        x = _layer_norm(DEEPNORM_ALPHA * x + ffn, ln2_g[l], ln2_b[l])
    return (x, jnp.stack(out_sa), jnp.stack(out_ca), jnp.stack(out_sb), jnp.stack(out_cb))


def setup_inputs(seed: int = 0) -> dict:
    key = jax.random.key(seed)
    ks = jax.random.split(key, 32)
    f32 = jnp.float32
    nrm = lambda k, shape, scale: jax.random.normal(k, shape, f32) * scale

    def dt_bias(k, shape):
        dt = jnp.exp(jax.random.uniform(k, shape, f32, np.log(1e-3), np.log(1e-1)))
        return dt + jnp.log(-jnp.expm1(-dt))

    return {
        'x_prompt': nrm(ks[0], (BATCH, SEQ, D_MODEL), 1.0),
        'x_sample': nrm(ks[1], (DEC_BATCH, DEC_SEQ, D_MODEL), 1.0),
        'state_gdn': nrm(ks[2], (DEPTH, DEC_BATCH, GDN_HEADS, GDN_DK, GDN_DV), 0.1),
        'state_gdn_conv': nrm(ks[3], (DEPTH, DEC_BATCH, CONV_W - 1, GDN_CONV_DIM), 1.0),
        'state_ssd': nrm(ks[4], (DEPTH, DEC_BATCH, SSD_HEADS, SSD_P, SSD_N), 0.1),
        'state_ssd_conv': nrm(ks[5], (DEPTH, DEC_BATCH, CONV_W - 1, SSD_CONV_DIM), 1.0),
        'meta_tokens': nrm(ks[6], (N_META, D_MODEL), 1.0),
        'w_in': nrm(ks[7], (DEPTH, D_MODEL, IN_DIM), D_MODEL ** -0.5),
        'gdn_conv_w': nrm(ks[8], (DEPTH, CONV_W, GDN_CONV_DIM), CONV_W ** -0.5),
        'gdn_a_log': jnp.log(jax.random.uniform(ks[9], (DEPTH, GDN_HEADS), f32, 1.0, 16.0)),
        'gdn_dt_bias': dt_bias(ks[10], (DEPTH, GDN_HEADS)),
        'gdn_norm_w': 1.0 + nrm(ks[11], (DEPTH, GDN_DV), 0.05),
        'ssd_conv_w': nrm(ks[12], (DEPTH, CONV_W, SSD_CONV_DIM), CONV_W ** -0.5),
        'ssd_conv_b': nrm(ks[13], (DEPTH, SSD_CONV_DIM), 0.02),
        'ssd_a_log': jnp.log(jax.random.uniform(ks[14], (DEPTH, SSD_HEADS), f32, 1.0, 16.0)),
        'ssd_dt_bias': dt_bias(ks[15], (DEPTH, SSD_HEADS)),
        'ssd_d': 1.0 + nrm(ks[16], (DEPTH, SSD_HEADS), 0.05),
        'ssd_norm_w': 1.0 + nrm(ks[17], (DEPTH, SSD_INNER), 0.05),
        'w_out': nrm(ks[18], (DEPTH, MIX_WIDTH, D_MODEL), MIX_WIDTH ** -0.5 * DEEPNORM_BETA),
        'ln1_g': 1.0 + nrm(ks[19], (DEPTH, D_MODEL), 0.05),
        'ln1_b': nrm(ks[20], (DEPTH, D_MODEL), 0.02),
        'router_g': nrm(ks[21], (DEPTH, D_MODEL, N_EGROUPS), D_MODEL ** -0.5),
        'router_e': nrm(ks[22], (DEPTH, D_MODEL, N_EXPERTS), D_MODEL ** -0.5),
        'w1': nrm(ks[23], (DEPTH, N_EXPERTS, D_MODEL, D_EXPERT), D_MODEL ** -0.5),
        'w3': nrm(ks[24], (DEPTH, N_EXPERTS, D_MODEL, D_EXPERT), D_MODEL ** -0.5),
        'w2': nrm(ks[25], (DEPTH, N_EXPERTS, D_EXPERT, D_MODEL), D_EXPERT ** -0.5 * DEEPNORM_BETA),
        'ln2_g': 1.0 + nrm(ks[26], (DEPTH, D_MODEL), 0.05),
        'ln2_b': nrm(ks[27], (DEPTH, D_MODEL), 0.02),
    }


def reference(x_prompt, x_sample, state_gdn, state_gdn_conv, state_ssd, state_ssd_conv, meta_tokens,
              w_in, gdn_conv_w, gdn_a_log, gdn_dt_bias, gdn_norm_w, ssd_conv_w, ssd_conv_b, ssd_a_log,
              ssd_dt_bias, ssd_d, ssd_norm_w, w_out, ln1_g, ln1_b, router_g, router_e, w1, w3, w2, ln2_g, ln2_b):
    params = (w_in, gdn_conv_w, gdn_a_log, gdn_dt_bias, gdn_norm_w, ssd_conv_w, ssd_conv_b, ssd_a_log,
              ssd_dt_bias, ssd_d, ssd_norm_w, w_out, ln1_g, ln1_b, router_g, router_e, w1, w3, w2, ln2_g, ln2_b)

    bp, sp = x_prompt.shape[0], x_prompt.shape[1]
    meta = jnp.broadcast_to(meta_tokens.astype(x_prompt.dtype), (bp, N_META, D_MODEL))
    xp = jnp.concatenate([meta, x_prompt], axis=1)
    zs_a = jnp.zeros((DEPTH, bp, GDN_HEADS, GDN_DK, GDN_DV), state_gdn.dtype)
    zc_a = jnp.zeros((DEPTH, bp, CONV_W - 1, GDN_CONV_DIM), state_gdn_conv.dtype)
    zs_b = jnp.zeros((DEPTH, bp, SSD_HEADS, SSD_P, SSD_N), state_ssd.dtype)
    zc_b = jnp.zeros((DEPTH, bp, CONV_W - 1, SSD_CONV_DIM), state_ssd_conv.dtype)
    yp, p_sa, p_ca, p_sb, p_cb = _trunk(xp, zs_a, zc_a, zs_b, zc_b, [(N_META, N_META), (sp, CHUNK)], params)
    y_prompt = yp[:, N_META:]

    ls = x_sample.shape[1]
    y_sample, s_sa, s_ca, s_sb, s_cb = _trunk(x_sample, state_gdn, state_gdn_conv, state_ssd, state_ssd_conv,
                                              [(ls, min(CHUNK, ls))], params)
    return (y_prompt, y_sample, p_sa, p_ca, p_sb, p_cb, s_sa, s_ca, s_sb, s_cb)
```

```python
import functools

import jax
import jax.numpy as jnp
import numpy as np
from jax import lax
from jax.experimental import pallas as pl
from jax.experimental.pallas import tpu as pltpu

F32 = jnp.float32
BF16 = jnp.bfloat16

N_META = 16
CONV_W = 4
GDN_HEADS = 4
GDN_DK = 128
GDN_DV = 128
SSD_HEADS = 8
SSD_P = 64
SSD_N = 128
SSD_GROUPS = 2
GDN_QK = GDN_HEADS * GDN_DK
GDN_V = GDN_HEADS * GDN_DV
SSD_INNER = SSD_HEADS * SSD_P
SSD_BC = SSD_GROUPS * SSD_N
GDN_CONV_DIM = 2 * GDN_QK + GDN_V
SSD_CONV_DIM = SSD_INNER + 2 * SSD_BC
N_EGROUPS = 4
E_PER_GROUP = 8
N_EXPERTS = N_EGROUPS * E_PER_GROUP
EPS = 1e-6

LANES = 128
SUBLANES = 8
VMEM_LIMIT_BYTES = 56 * 1024 * 1024

SM_BETA = 0
SM_GA = GDN_HEADS
SM_DT = 2 * GDN_HEADS
SM_WIDTH = LANES

PROMPT_CHUNK = 64
ROUTE_WIDTH = LANES
MOE_ROWS = 256


def _cparams(sem, vmem=VMEM_LIMIT_BYTES):
    return pltpu.CompilerParams(dimension_semantics=sem, vmem_limit_bytes=vmem)


def _tile(n, pref):
    t = min(pref, n)
    t -= t % SUBLANES
    while t > SUBLANES and n % t:
        t -= SUBLANES
    assert t >= SUBLANES and n % t == 0, (n, pref)
    return t


def _mm(a, b):
    return jnp.dot(a.astype(BF16), b.astype(BF16), preferred_element_type=F32)


def _mm_nt(a, b):
    return lax.dot_general(a.astype(BF16), b.astype(BF16), (((1,), (1,)), ((), ())),
                           preferred_element_type=F32)


def _mm_tn(a, b):
    return lax.dot_general(a.astype(BF16), b.astype(BF16), (((0,), (0,)), ((), ())),
                           preferred_element_type=F32)


def _silu(x):
    return x * (1.0 / (1.0 + jnp.exp(-x)))


def _sigmoid(x):
    return 1.0 / (1.0 + jnp.exp(-x))


def _softplus(x):
    return jnp.maximum(x, 0.0) + jnp.log(1.0 + jnp.exp(-jnp.abs(x)))


def _inproj_kernel(x_ref, w_ref, qkv_ref, xbc_ref, za_ref, zb_ref, sm_ref):
    p = jnp.dot(x_ref[...].astype(BF16), w_ref[...], preferred_element_type=F32)
    c0 = GDN_CONV_DIM
    c1 = c0 + SSD_CONV_DIM
    c2 = c1 + GDN_V
    c3 = c2 + SSD_INNER
    qkv_ref[...] = p[:, :c0]
    xbc_ref[...] = p[:, c0:c1]
    za_ref[...] = p[:, c1:c2]
    zb_ref[...] = p[:, c2:c3]
    sm_ref[...] = p[:, c3:c3 + SM_WIDTH]


def _inproj(x, w_cat):
    n, d = x.shape
    tm = _tile(n, 256)
    widths = (GDN_CONV_DIM, SSD_CONV_DIM, GDN_V, SSD_INNER, SM_WIDTH)
    return pl.pallas_call(
        _inproj_kernel,
        grid=(n // tm,),
        in_specs=[pl.BlockSpec((tm, d), lambda i: (i, 0)),
                  pl.BlockSpec(w_cat.shape, lambda i: (0, 0))],
        out_specs=[pl.BlockSpec((tm, w), lambda i: (i, 0)) for w in widths],
        out_shape=[jax.ShapeDtypeStruct((n, w), F32) for w in widths],
        compiler_params=_cparams(("parallel",)),
    )(x, w_cat)


def _conv_silu(xp_ref, w_ref, bias, row0, rows, lane0, width):
    win = xp_ref[pl.ds(row0, rows + SUBLANES), lane0:lane0 + width]
    w = w_ref[:, lane0:lane0 + width]
    acc = None
    for i in range(CONV_W):
        off = SUBLANES - (CONV_W - 1) + i
        term = win[off:off + rows, :] * w[i:i + 1, :]
        acc = term if acc is None else acc + term
    if bias is not None:
        acc = acc + bias
    return _silu(acc)


def _masks(c):
    r = lax.broadcasted_iota(jnp.int32, (c, c), 0)
    q = lax.broadcasted_iota(jnp.int32, (c, c), 1)
    return q <= r, q < r, q == r


def _decay_terms(g_col, tril, eye):
    g_row = jnp.sum(jnp.where(eye, g_col, 0.0), axis=0, keepdims=True)
    gam_col = jnp.sum(jnp.where(tril, g_row, 0.0), axis=1, keepdims=True)
    gam_row = jnp.sum(jnp.where(eye, gam_col, 0.0), axis=0, keepdims=True)
    dec = jnp.exp(jnp.where(tril, gam_col - gam_row, -1e30))
    c = g_col.shape[0]
    return gam_col, gam_col[c - 1:c, :], dec


def _unit_lower_inverse(a):
    c = a.shape[0]
    _, _, eye = _masks(c)
    t = jnp.where(eye, 1.0, 0.0) - a
    p = a
    steps = max(int(np.ceil(np.log2(c))) - 1, 0)
    for _ in range(steps):
        p = _mm(p, p)
        t = t + _mm(t, p)
    return t


def _gdn_kernel(*refs, chunk, n_chunks, seq_len, has_state):
    if has_state:
        (qkv_ref, sm_ref, za_ref, cw_ref, bias_ref, alog_ref, nw_ref, s0_ref, tail0_ref,
         o_ref, sout_ref, xp_scr, s_scr, g_scr, b_scr) = refs
    else:
        (qkv_ref, sm_ref, za_ref, cw_ref, bias_ref, alog_ref, nw_ref,
         o_ref, sout_ref, xp_scr, s_scr, g_scr, b_scr) = refs
    j = pl.program_id(1)
    rows = chunk * n_chunks

    @pl.when(j == 0)
    def _():
        if has_state:
            xp_scr[0:SUBLANES, :] = tail0_ref[0]
            s_scr[...] = s0_ref[0]
        else:
            xp_scr[0:SUBLANES, :] = jnp.zeros((SUBLANES, GDN_CONV_DIM), F32)
            s_scr[...] = jnp.zeros(s_scr.shape, F32)

    xp_scr[SUBLANES:SUBLANES + rows, :] = qkv_ref[...]

    sm = sm_ref[...]
    g_all = -jnp.exp(alog_ref[...]) * _softplus(sm + bias_ref[...])
    beta_all = _sigmoid(sm)
    valid = j * rows + lax.broadcasted_iota(jnp.int32, (rows, 1), 0) < seq_len
    g_all = jnp.where(valid, g_all, 0.0)
    beta_all = jnp.where(valid, beta_all, 0.0)
    g_scr[...] = g_all
    b_scr[...] = beta_all

    tril, tril_s, eye = _masks(chunk)
    nw = nw_ref[...]

    def chunk_body(c, carry):
        r0 = pl.multiple_of(c * chunk, SUBLANES)
        gsl = g_scr[pl.ds(r0, chunk), :]
        bsl = b_scr[pl.ds(r0, chunk), :]
        for h in range(GDN_HEADS):
            q = _conv_silu(xp_scr, cw_ref, None, r0, chunk, h * GDN_DK, GDN_DK)
            k = _conv_silu(xp_scr, cw_ref, None, r0, chunk, GDN_QK + h * GDN_DK, GDN_DK)
            v = _conv_silu(xp_scr, cw_ref, None, r0, chunk, 2 * GDN_QK + h * GDN_DV, GDN_DV)
            q = q * lax.rsqrt(jnp.sum(q * q, axis=1, keepdims=True) + EPS) * (GDN_DK ** -0.5)
            k = k * lax.rsqrt(jnp.sum(k * k, axis=1, keepdims=True) + EPS)
            g_col = gsl[:, SM_GA + h:SM_GA + h + 1]
            beta = bsl[:, SM_BETA + h:SM_BETA + h + 1]
            gam, gam_last, dec = _decay_terms(g_col, tril, eye)
            eg = jnp.exp(gam)
            kk = _mm_nt(k, k)
            qk = _mm_nt(q, k) * dec
            a = jnp.where(tril_s, beta * kk * dec, 0.0)
            t_inv = _unit_lower_inverse(a)
            s = s_scr[h]
            kqs = _mm(jnp.concatenate([k * eg, q * eg], axis=0), s)
            rhs = beta * (v - kqs[:chunk])
            u = _mm(t_inv, rhs)
            o = kqs[chunk:] + _mm(qk, u)
            s_scr[h] = jnp.exp(gam_last) * s + _mm_tn(k * jnp.exp(gam_last - gam), u)
            o = o * lax.rsqrt(jnp.mean(o * o, axis=1, keepdims=True) + EPS) * nw
            z = za_ref[pl.ds(r0, chunk), h * GDN_DV:(h + 1) * GDN_DV]
            o_ref[pl.ds(r0, chunk), h * GDN_DV:(h + 1) * GDN_DV] = o * _silu(z)
        return carry

    lax.fori_loop(0, n_chunks, chunk_body, 0)
    xp_scr[0:SUBLANES, :] = xp_scr[rows:rows + SUBLANES, :]

    @pl.when(j == pl.num_programs(1) - 1)
    def _():
        sout_ref[0] = s_scr[...]


def _gdn(qkv, sm, za, cw, bias_vec, alog_vec, nw, *, row0, n_seq, seq_rows, seq_len, chunk,
         step_rows, state=None, tail=None):
    n = qkv.shape[0]
    steps = seq_rows // step_rows
    blk0 = row0 // step_rows
    assert row0 % step_rows == 0 and seq_rows % step_rows == 0 and step_rows % chunk == 0
    rmap = lambda b, j: (blk0 + b * steps + j, 0)
    cmap = lambda b, j: (0, 0)
    in_specs = [pl.BlockSpec((step_rows, GDN_CONV_DIM), rmap),
                pl.BlockSpec((step_rows, SM_WIDTH), rmap),
                pl.BlockSpec((step_rows, GDN_V), rmap),
                pl.BlockSpec(cw.shape, cmap),
                pl.BlockSpec(bias_vec.shape, cmap),
                pl.BlockSpec(alog_vec.shape, cmap),
                pl.BlockSpec(nw.shape, cmap)]
    args = [qkv, sm, za, cw, bias_vec, alog_vec, nw]
    has_state = state is not None
    if has_state:
        in_specs += [pl.BlockSpec((1, GDN_HEADS, GDN_DK, GDN_DV), lambda b, j: (b, 0, 0, 0)),
                     pl.BlockSpec((1, SUBLANES, GDN_CONV_DIM), lambda b, j: (b, 0, 0))]
        args += [state, tail]
    kern = functools.partial(_gdn_kernel, chunk=chunk, n_chunks=step_rows // chunk,
                             seq_len=seq_len, has_state=has_state)
    o, s_out = pl.pallas_call(
        kern,
        grid=(n_seq, steps),
        in_specs=in_specs,
        out_specs=[pl.BlockSpec((step_rows, GDN_V), lambda b, j: (b * steps + j, 0)),
                   pl.BlockSpec((1, GDN_HEADS, GDN_DK, GDN_DV), lambda b, j: (b, 0, 0, 0))],
        out_shape=[jax.ShapeDtypeStruct((n_seq * seq_rows, GDN_V), F32),
                   jax.ShapeDtypeStruct((n_seq, GDN_HEADS, GDN_DK, GDN_DV), F32)],
        scratch_shapes=[pltpu.VMEM((step_rows + SUBLANES, GDN_CONV_DIM), F32),
                        pltpu.VMEM((GDN_HEADS, GDN_DK, GDN_DV), F32),
                        pltpu.VMEM((step_rows, SM_WIDTH), F32),
                        pltpu.VMEM((step_rows, SM_WIDTH), F32)],
        compiler_params=_cparams(("parallel", "arbitrary")),
    )(*args)
    return o, s_out


def _ssd_kernel(*refs, chunk, n_chunks, seq_len, has_state):
    if has_state:
        (xbc_ref, sm_ref, zb_ref, cw_ref, cb_ref, bias_ref, alog_ref, d_ref, nw_ref, s0_ref, tail0_ref,
         o_ref, sout_ref, xp_scr, s_scr, g_scr, dt_scr, y_scr) = refs
    else:
        (xbc_ref, sm_ref, zb_ref, cw_ref, cb_ref, bias_ref, alog_ref, d_ref, nw_ref,
         o_ref, sout_ref, xp_scr, s_scr, g_scr, dt_scr, y_scr) = refs
    j = pl.program_id(1)
    rows = chunk * n_chunks

    @pl.when(j == 0)
    def _():
        if has_state:
            xp_scr[0:SUBLANES, :] = tail0_ref[0]
            s_scr[...] = s0_ref[0]
        else:
            xp_scr[0:SUBLANES, :] = jnp.zeros((SUBLANES, SSD_CONV_DIM), F32)
            s_scr[...] = jnp.zeros(s_scr.shape, F32)

    xp_scr[SUBLANES:SUBLANES + rows, :] = xbc_ref[...]

    sm = sm_ref[...]
    dt_all = _softplus(sm + bias_ref[...])
    ridx = j * rows + lax.broadcasted_iota(jnp.int32, (rows, 1), 0)
    dt_all = jnp.where(ridx < seq_len, dt_all, 0.0)
    g_scr[...] = -jnp.exp(alog_ref[...]) * dt_all
    dt_scr[...] = dt_all

    tril, _, eye = _masks(chunk)
    hpg = SSD_HEADS // SSD_GROUPS
    gw = SSD_INNER // SSD_GROUPS

    def chunk_body(c, carry):
        r0 = pl.multiple_of(c * chunk, SUBLANES)
        gsl = g_scr[pl.ds(r0, chunk), :]
        dsl = dt_scr[pl.ds(r0, chunk), :]
        xs = _conv_silu(xp_scr, cw_ref, cb_ref[:, 0:SSD_INNER], r0, chunk, 0, SSD_INNER)
        for grp in range(SSD_GROUPS):
            lb = SSD_INNER + grp * SSD_N
            lc = SSD_INNER + SSD_BC + grp * SSD_N
            bm = _conv_silu(xp_scr, cw_ref, cb_ref[:, lb:lb + SSD_N], r0, chunk, lb, SSD_N)
            cm = _conv_silu(xp_scr, cw_ref, cb_ref[:, lc:lc + SSD_N], r0, chunk, lc, SSD_N)
            cbt = _mm_nt(cm, bm)
            for hh in range(hpg):
                h = grp * hpg + hh
                g_col = gsl[:, SM_DT + h:SM_DT + h + 1]
                dt = dsl[:, SM_DT + h:SM_DT + h + 1]
                gam, gam_last, dec = _decay_terms(g_col, tril, eye)
                x_h = xs[:, h * SSD_P:(h + 1) * SSD_P]
                xdt = x_h * dt
                hs = s_scr[h]
                y = _mm_nt(cm * jnp.exp(gam), hs) + _mm(cbt * dec, xdt)
                s_scr[h] = jnp.exp(gam_last) * hs + _mm_tn(xdt, bm * jnp.exp(gam_last - gam))
                y_scr[:, h * SSD_P:(h + 1) * SSD_P] = y
        y = y_scr[...] + d_ref[...] * xs
        y = y * _silu(zb_ref[pl.ds(r0, chunk), :])
        nw = nw_ref[...]
        for grp in range(SSD_GROUPS):
            yg = y[:, grp * gw:(grp + 1) * gw]
            yg = yg * lax.rsqrt(jnp.mean(yg * yg, axis=1, keepdims=True) + EPS)
            o_ref[pl.ds(r0, chunk), grp * gw:(grp + 1) * gw] = yg * nw[:, grp * gw:(grp + 1) * gw]
        return carry

    lax.fori_loop(0, n_chunks, chunk_body, 0)
    xp_scr[0:SUBLANES, :] = xp_scr[rows:rows + SUBLANES, :]

    @pl.when(j == pl.num_programs(1) - 1)
    def _():
        sout_ref[0] = s_scr[...]


def _ssd(xbc, sm, zb, cw, cb, bias_vec, alog_vec, d_vec, nw, *, row0, n_seq, seq_rows, seq_len, chunk,
         step_rows, state=None, tail=None):
    steps = seq_rows // step_rows
    blk0 = row0 // step_rows
    assert row0 % step_rows == 0 and seq_rows % step_rows == 0 and step_rows % chunk == 0
    rmap = lambda b, j: (blk0 + b * steps + j, 0)
    cmap = lambda b, j: (0, 0)
    in_specs = [pl.BlockSpec((step_rows, SSD_CONV_DIM), rmap),
                pl.BlockSpec((step_rows, SM_WIDTH), rmap),
                pl.BlockSpec((step_rows, SSD_INNER), rmap)]
    args = [xbc, sm, zb]
    for a in (cw, cb, bias_vec, alog_vec, d_vec, nw):
        in_specs.append(pl.BlockSpec(a.shape, cmap))
        args.append(a)
    has_state = state is not None
    if has_state:
        in_specs += [pl.BlockSpec((1, SSD_HEADS, SSD_P, SSD_N), lambda b, j: (b, 0, 0, 0)),
                     pl.BlockSpec((1, SUBLANES, SSD_CONV_DIM), lambda b, j: (b, 0, 0))]
        args += [state, tail]
    kern = functools.partial(_ssd_kernel, chunk=chunk, n_chunks=step_rows // chunk,
                             seq_len=seq_len, has_state=has_state)
    o, s_out = pl.pallas_call(
        kern,
        grid=(n_seq, steps),
        in_specs=in_specs,
        out_specs=[pl.BlockSpec((step_rows, SSD_INNER), lambda b, j: (b * steps + j, 0)),
                   pl.BlockSpec((1, SSD_HEADS, SSD_P, SSD_N), lambda b, j: (b, 0, 0, 0))],
        out_shape=[jax.ShapeDtypeStruct((n_seq * seq_rows, SSD_INNER), F32),
                   jax.ShapeDtypeStruct((n_seq, SSD_HEADS, SSD_P, SSD_N), F32)],
        scratch_shapes=[pltpu.VMEM((step_rows + SUBLANES, SSD_CONV_DIM), F32),
                        pltpu.VMEM((SSD_HEADS, SSD_P, SSD_N), F32),
                        pltpu.VMEM((step_rows, SM_WIDTH), F32),
                        pltpu.VMEM((step_rows, SM_WIDTH), F32),
                        pltpu.VMEM((chunk, SSD_INNER), F32)],
        compiler_params=_cparams(("parallel", "arbitrary")),
    )(*args)
    return o, s_out


def _layer_norm(x, g, b):
    mu = jnp.mean(x, axis=1, keepdims=True)
    xc = x - mu
    var = jnp.mean(xc * xc, axis=1, keepdims=True)
    return xc * lax.rsqrt(var + EPS) * g + b


def _first_lane_where(cond, lane):
    return jnp.min(jnp.where(cond, lane, float(ROUTE_WIDTH)), axis=1, keepdims=True)


def _outproj_route_kernel(oa_ref, yb_ref, x_ref, wo_ref, g_ref, b_ref, wr_ref,
                          h_ref, route_ref, cnt_ref, carry_scr, *, alpha):
    i = pl.program_id(0)
    tm = x_ref.shape[0]

    @pl.when(i == 0)
    def _():
        carry_scr[...] = jnp.zeros(carry_scr.shape, F32)

    mix = (jnp.dot(oa_ref[...].astype(BF16), wo_ref[0:GDN_V, :], preferred_element_type=F32)
           + jnp.dot(yb_ref[...].astype(BF16), wo_ref[GDN_V:, :], preferred_element_type=F32))
    h = _layer_norm(alpha * x_ref[...] + mix, g_ref[...], b_ref[...])
    h_ref[...] = h

    logits = jnp.dot(h, wr_ref[...], preferred_element_type=F32, precision=lax.Precision.HIGHEST)
    lane = lax.broadcasted_iota(jnp.int32, (tm, ROUTE_WIDTH), 1).astype(F32)
    neg = -jnp.inf
    is_g = lane < N_EGROUPS
    lg = jnp.where(is_g, logits, neg)
    mg = jnp.max(lg, axis=1, keepdims=True)
    gi = _first_lane_where(is_g & (lg == mg), lane)
    g_val = 1.0 / jnp.sum(jnp.exp(lg - mg), axis=1, keepdims=True)
    e_lo = N_EGROUPS + gi * E_PER_GROUP
    in_grp = (lane >= e_lo) & (lane < e_lo + E_PER_GROUP)
    l1 = jnp.where(in_grp, logits, neg)
    m1 = jnp.max(l1, axis=1, keepdims=True)
    i1 = _first_lane_where(in_grp & (l1 == m1), lane)
    l2 = jnp.where(in_grp & (lane != i1), logits, neg)
    m2 = jnp.max(l2, axis=1, keepdims=True)
    i2 = _first_lane_where((l2 == m2) & in_grp & (lane != i1), lane)
    p2 = jnp.exp(m2 - m1)
    gate1 = g_val / (1.0 + p2)
    gate2 = g_val * p2 / (1.0 + p2)
    e1 = i1 - N_EGROUPS
    e2 = i2 - N_EGROUPS

    oh1 = lane == e1
    oh2 = lane == e2
    r = lax.broadcasted_iota(jnp.int32, (tm, tm), 0)
    q = lax.broadcasted_iota(jnp.int32, (tm, tm), 1)
    below = jnp.where(q < r, 1.0, 0.0).astype(BF16)
    c1 = jnp.dot(below, jnp.where(oh1, 1.0, 0.0).astype(BF16), preferred_element_type=F32)
    c2 = jnp.dot(below, jnp.where(oh2, 1.0, 0.0).astype(BF16), preferred_element_type=F32)
    tot1 = jnp.sum(jnp.where(oh1, 1.0, 0.0), axis=0, keepdims=True)
    tot2 = jnp.sum(jnp.where(oh2, 1.0, 0.0), axis=0, keepdims=True)
    carry = carry_scr[...]
    rank1 = jnp.sum(jnp.where(oh1, carry + c1, 0.0), axis=1, keepdims=True)
    rank2 = jnp.sum(jnp.where(oh2, carry + tot1 + c2, 0.0), axis=1, keepdims=True)
    carry = carry + tot1 + tot2
    carry_scr[...] = carry
    cnt_ref[...] = carry

    rec = jnp.where(lane == 0, e1, 0.0)
    rec = jnp.where(lane == 1, e2, rec)
    rec = jnp.where(lane == 2, rank1, rec)
    rec = jnp.where(lane == 3, rank2, rec)
    rec = jnp.where(lane == 4, gate1, rec)
    rec = jnp.where(lane == 5, gate2, rec)
    route_ref[...] = rec


def _outproj_route(oa, yb, x, wo, g, b, wr, alpha):
    n, d = x.shape
    tm = _tile(n, 256)
    kern = functools.partial(_outproj_route_kernel, alpha=alpha)
    rmap = lambda i: (i, 0)
    cmap = lambda i: (0, 0)
    return pl.pallas_call(
        kern,
        grid=(n // tm,),
        in_specs=[pl.BlockSpec((tm, GDN_V), rmap), pl.BlockSpec((tm, SSD_INNER), rmap),
                  pl.BlockSpec((tm, d), rmap), pl.BlockSpec(wo.shape, cmap),
                  pl.BlockSpec(g.shape, cmap), pl.BlockSpec(b.shape, cmap), pl.BlockSpec(wr.shape, cmap)],
        out_specs=[pl.BlockSpec((tm, d), rmap), pl.BlockSpec((tm, ROUTE_WIDTH), rmap),
                   pl.BlockSpec((1, ROUTE_WIDTH), cmap)],
        out_shape=[jax.ShapeDtypeStruct((n, d), F32), jax.ShapeDtypeStruct((n, ROUTE_WIDTH), F32),
                   jax.ShapeDtypeStruct((1, ROUTE_WIDTH), F32)],
        scratch_shapes=[pltpu.VMEM((1, ROUTE_WIDTH), F32)],
        compiler_params=_cparams(("arbitrary",)),
    )(oa, yb, x, wo, g, b, wr)


def _row_copy(src_ref, src_row, dst_ref, dst_row, sem):
    return pltpu.make_async_copy(src_ref.at[pl.ds(src_row, 1), :], dst_ref.at[pl.ds(dst_row, 1), :], sem)


def _dispatch_kernel(dest_ref, h_ref, slots_in_ref, slots_ref, sem):
    del slots_in_ref
    tm = h_ref.shape[0]

    def issue(t, carry):
        for k in range(2):
            _row_copy(h_ref, t, slots_ref, dest_ref[0, 0, 2 * t + k], sem).start()
        return carry

    lax.fori_loop(0, tm, issue, 0)

    def drain(t, carry):
        for k in range(2):
            _row_copy(h_ref, t, slots_ref, dest_ref[0, 0, 2 * t + k], sem).wait()
        return carry

    lax.fori_loop(0, tm, drain, 0)


def _dispatch(h, dest, n_slots):
    n, d = h.shape
    tm = _tile(n, 256)
    dest3 = dest.reshape(n // tm, 1, 2 * tm)
    slots0 = jnp.zeros((n_slots, d), F32)
    return pl.pallas_call(
        _dispatch_kernel,
        grid=(n // tm,),
        in_specs=[pl.BlockSpec((1, 1, 2 * tm), lambda i: (i, 0, 0), memory_space=pltpu.SMEM),
                  pl.BlockSpec((tm, d), lambda i: (i, 0)),
                  pl.BlockSpec(memory_space=pl.ANY)],
        out_specs=pl.BlockSpec(memory_space=pl.ANY),
        out_shape=jax.ShapeDtypeStruct((n_slots, d), F32),
        scratch_shapes=[pltpu.SemaphoreType.DMA(())],
        input_output_aliases={2: 0},
        compiler_params=_cparams(("arbitrary",)),
    )(dest3, h, slots0)


def _moe_kernel(bexp_ref, nused_ref, xs_ref, w1_ref, w3_ref, w2_ref, ys_ref, w1b, w3b, w2b):
    i = pl.program_id(0)

    @pl.when(i < nused_ref[0])
    def _():
        prev = bexp_ref[jnp.maximum(i - 1, 0)]
        fresh = jnp.logical_or(i == 0, bexp_ref[i] != prev)

        @pl.when(fresh)
        def _():
            w1b[...] = w1_ref[0, 0].astype(BF16)
            w3b[...] = w3_ref[0, 0].astype(BF16)
            w2b[...] = w2_ref[0, 0].astype(BF16)

        x = xs_ref[...].astype(BF16)
        a = jnp.dot(x, w1b[...], preferred_element_type=F32)
        b = jnp.dot(x, w3b[...], preferred_element_type=F32)
        hmid = (_silu(a) * b).astype(BF16)
        ys_ref[...] = jnp.dot(hmid, w2b[...], preferred_element_type=F32)

    @pl.when(i >= nused_ref[0])
    def _():
        ys_ref[...] = jnp.zeros(ys_ref.shape, F32)


def _moe(slots, w1, w3, w2, layer, block_exp, n_used):
    n_slots, d = slots.shape
    nb = n_slots // MOE_ROWS
    de = w1.shape[-1]
    row_map = lambda i, be, nu: (jnp.minimum(i, nu[0] - 1), 0)
    grid_spec = pltpu.PrefetchScalarGridSpec(
        num_scalar_prefetch=2,
        grid=(nb,),
        in_specs=[pl.BlockSpec((MOE_ROWS, d), row_map),
                  pl.BlockSpec((1, 1, d, de), lambda i, be, nu: (layer, be[i], 0, 0)),
                  pl.BlockSpec((1, 1, d, de), lambda i, be, nu: (layer, be[i], 0, 0)),
                  pl.BlockSpec((1, 1, de, d), lambda i, be, nu: (layer, be[i], 0, 0))],
        out_specs=pl.BlockSpec((MOE_ROWS, d), lambda i, be, nu: (i, 0)),
        scratch_shapes=[pltpu.VMEM((d, de), BF16), pltpu.VMEM((d, de), BF16), pltpu.VMEM((de, d), BF16)],
    )
    return pl.pallas_call(
        _moe_kernel,
        grid_spec=grid_spec,
        out_shape=jax.ShapeDtypeStruct((n_slots, d), F32),
        compiler_params=_cparams(("arbitrary",)),
    )(block_exp, n_used, slots, w1, w3, w2)


def _combine_kernel(dest_ref, route_ref, h_ref, ys_ref, g_ref, b_ref, o_ref, rows_scr, sem, *, alpha):
    tm = h_ref.shape[0]

    def issue(t, carry):
        for k in range(2):
            _row_copy(ys_ref, dest_ref[0, 0, 2 * t + k], rows_scr.at[k], t, sem).start()
        return carry

    lax.fori_loop(0, tm, issue, 0)

    def drain(t, carry):
        for k in range(2):
            _row_copy(ys_ref, dest_ref[0, 0, 2 * t + k], rows_scr.at[k], t, sem).wait()
        return carry

    lax.fori_loop(0, tm, drain, 0)

    route = route_ref[...]
    y = route[:, 4:5] * rows_scr[0] + route[:, 5:6] * rows_scr[1]
    o_ref[...] = _layer_norm(alpha * h_ref[...] + y, g_ref[...], b_ref[...])


def _combine(h, route, dest, ys, g, b, alpha):
    n, d = h.shape
    tm = _tile(n, 256)
    dest3 = dest.reshape(n // tm, 1, 2 * tm)
    kern = functools.partial(_combine_kernel, alpha=alpha)
    rmap = lambda i: (i, 0)
    cmap = lambda i: (0, 0)
    return pl.pallas_call(
        kern,
        grid=(n // tm,),
        in_specs=[pl.BlockSpec((1, 1, 2 * tm), lambda i: (i, 0, 0), memory_space=pltpu.SMEM),
                  pl.BlockSpec((tm, ROUTE_WIDTH), rmap),
                  pl.BlockSpec((tm, d), rmap),
                  pl.BlockSpec(memory_space=pl.ANY),
                  pl.BlockSpec(g.shape, cmap), pl.BlockSpec(b.shape, cmap)],
        out_specs=pl.BlockSpec((tm, d), rmap),
        out_shape=jax.ShapeDtypeStruct((n, d), F32),
        scratch_shapes=[pltpu.VMEM((2, tm, d), F32), pltpu.SemaphoreType.DMA(())],
        compiler_params=_cparams(("arbitrary",)),
    )(dest3, route, h, ys, g, b)


def _pad_lanes(v, width=SM_WIDTH, at=0):
    out = jnp.zeros((1, width), F32)
    return lax.dynamic_update_slice(out, v.reshape(1, -1).astype(F32), (0, at))


def kernel(x_prompt, x_sample, state_gdn, state_gdn_conv, state_ssd, state_ssd_conv, meta_tokens, w_in, gdn_conv_w, gdn_a_log, gdn_dt_bias, gdn_norm_w, ssd_conv_w, ssd_conv_b, ssd_a_log, ssd_dt_bias, ssd_d, ssd_norm_w, w_out, ln1_g, ln1_b, router_g, router_e, w1, w3, w2, ln2_g, ln2_b):
    depth, d = w_in.shape[0], w_in.shape[1]
    bp, sp = x_prompt.shape[0], x_prompt.shape[1]
    bs, ls = x_sample.shape[0], x_sample.shape[1]
    alpha = (2 * depth) ** 0.25
    lp = N_META + sp
    lp_pad = -(-lp // PROMPT_CHUNK) * PROMPT_CHUNK
    n_p = bp * lp_pad
    n_s = bs * ls
    n = n_p + n_s

    meta = jnp.broadcast_to(meta_tokens.astype(x_prompt.dtype), (bp, N_META, d))
    xp = jnp.concatenate([meta, x_prompt, jnp.zeros((bp, lp_pad - lp, d), x_prompt.dtype)], axis=1)
    x = jnp.concatenate([xp.reshape(n_p, d), x_sample.reshape(n_s, d)], axis=0)

    p_chunks = lp_pad // PROMPT_CHUNK
    per_step = max(c for c in range(1, 13) if p_chunks % c == 0)
    p_step = per_step * PROMPT_CHUNK
    assert n_p % ls == 0 and n_p % p_step == 0

    cuts = np.cumsum([GDN_CONV_DIM, GDN_V, GDN_HEADS, GDN_HEADS, SSD_INNER, SSD_CONV_DIM]).tolist()
    w_qkv, w_za, w_b, w_a, w_zb, w_xbc, w_dt = jnp.split(w_in, cuts, axis=-1)
    n_small = 2 * GDN_HEADS + SSD_HEADS
    w_cat = jnp.concatenate(
        [w_qkv, w_xbc, w_za, w_zb, w_b, w_a, w_dt, jnp.zeros((depth, d, SM_WIDTH - n_small), w_in.dtype)],
        axis=-1).astype(BF16)

    n_route = N_EGROUPS + N_EXPERTS
    w_route = jnp.concatenate(
        [router_g, router_e, jnp.zeros((depth, d, ROUTE_WIDTH - n_route), router_g.dtype)], axis=-1).astype(F32)
    w_out_b = w_out.astype(BF16)

    tail_gdn = jnp.pad(state_gdn_conv, ((0, 0), (0, 0), (SUBLANES - (CONV_W - 1), 0), (0, 0)))
    tail_ssd = jnp.pad(state_ssd_conv, ((0, 0), (0, 0), (SUBLANES - (CONV_W - 1), 0), (0, 0)))

    n_blocks = -(-2 * n // MOE_ROWS) + N_EXPERTS
    n_slots = n_blocks * MOE_ROWS

    outs = {k: [] for k in ("p_sa", "p_ca", "p_sb", "p_cb", "s_sa", "s_ca", "s_sb", "s_cb")}
    for l in range(depth):
        qkv, xbc, za, zb, sm = _inproj(x, w_cat[l])

        bias_vec = (_pad_lanes(gdn_dt_bias[l], at=SM_GA) + _pad_lanes(ssd_dt_bias[l], at=SM_DT))
        alog_vec = (_pad_lanes(gdn_a_log[l], at=SM_GA) + _pad_lanes(ssd_a_log[l], at=SM_DT))
        gnw = gdn_norm_w[l].reshape(1, GDN_DV).astype(F32)
        gcw = gdn_conv_w[l].astype(F32)
        scw = ssd_conv_w[l].astype(F32)
        scb = ssd_conv_b[l].reshape(1, SSD_CONV_DIM).astype(F32)
        d_vec = jnp.repeat(ssd_d[l].astype(F32), SSD_P).reshape(1, SSD_INNER)
        snw = ssd_norm_w[l].reshape(1, SSD_INNER).astype(F32)

        oa_p, sa_p = _gdn(qkv, sm, za, gcw, bias_vec, alog_vec, gnw, row0=0, n_seq=bp, seq_rows=lp_pad,
                          seq_len=lp, chunk=PROMPT_CHUNK, step_rows=p_step)
        oa_s, sa_s = _gdn(qkv, sm, za, gcw, bias_vec, alog_vec, gnw, row0=n_p, n_seq=bs, seq_rows=ls,
                          seq_len=ls, chunk=ls, step_rows=ls, state=state_gdn[l].astype(F32), tail=tail_gdn[l])
        yb_p, sb_p = _ssd(xbc, sm, zb, scw, scb, bias_vec, alog_vec, d_vec, snw, row0=0, n_seq=bp,
                          seq_rows=lp_pad, seq_len=lp, chunk=PROMPT_CHUNK, step_rows=p_step)
        yb_s, sb_s = _ssd(xbc, sm, zb, scw, scb, bias_vec, alog_vec, d_vec, snw, row0=n_p, n_seq=bs,
                          seq_rows=ls, seq_len=ls, chunk=ls, step_rows=ls,
                          state=state_ssd[l].astype(F32), tail=tail_ssd[l])
        oa = jnp.concatenate([oa_p, oa_s], axis=0)
        yb = jnp.concatenate([yb_p, yb_s], axis=0)

        qkv_p = qkv[:n_p].reshape(bp, lp_pad, GDN_CONV_DIM)
        xbc_p = xbc[:n_p].reshape(bp, lp_pad, SSD_CONV_DIM)
        outs["p_sa"].append(sa_p)
        outs["p_sb"].append(sb_p)
        outs["p_ca"].append(qkv_p[:, lp - (CONV_W - 1):lp])
        outs["p_cb"].append(xbc_p[:, lp - (CONV_W - 1):lp])
        outs["s_sa"].append(sa_s)
        outs["s_sb"].append(sb_s)
        outs["s_ca"].append(qkv[n_p:].reshape(bs, ls, GDN_CONV_DIM)[:, ls - (CONV_W - 1):])
        outs["s_cb"].append(xbc[n_p:].reshape(bs, ls, SSD_CONV_DIM)[:, ls - (CONV_W - 1):])

        h, route, counts = _outproj_route(oa, yb, x, w_out_b[l], ln1_g[l].reshape(1, d).astype(F32),
                                          ln1_b[l].reshape(1, d).astype(F32), w_route[l], alpha)

        cnt = counts[0, :N_EXPERTS].astype(jnp.int32)
        padded = ((cnt + MOE_ROWS - 1) // MOE_ROWS) * MOE_ROWS
        pad_end = jnp.cumsum(padded)
        pad_start = pad_end - padded
        e_idx = route[:, 0:2].astype(jnp.int32)
        rank = route[:, 2:4].astype(jnp.int32)
        onehot = e_idx[:, :, None] == jnp.arange(N_EXPERTS, dtype=jnp.int32)
        dest = (rank + jnp.sum(jnp.where(onehot, pad_start, 0), axis=-1)).reshape(-1)
        n_used = (pad_end[-1] // MOE_ROWS).astype(jnp.int32).reshape(1)
        blk_start = jnp.arange(n_blocks, dtype=jnp.int32) * MOE_ROWS
        block_exp = jnp.sum(blk_start[:, None] >= pad_end[None, :], axis=1).astype(jnp.int32)
        last_exp = jnp.max(jnp.where(cnt > 0, jnp.arange(N_EXPERTS, dtype=jnp.int32), 0))
        block_exp = jnp.minimum(block_exp, last_exp)

        slots = _dispatch(h, dest, n_slots)
        ys = _moe(slots, w1, w3, w2, l, block_exp, n_used)
        x = _combine(h, route, dest, ys, ln2_g[l].reshape(1, d).astype(F32),
                     ln2_b[l].reshape(1, d).astype(F32), alpha)

    y_prompt = x[:n_p].reshape(bp, lp_pad, d)[:, N_META:lp]
    y_sample = x[n_p:].reshape(bs, ls, d)
    st = {k: jnp.stack(v) for k, v in outs.items()}
    return (y_prompt, y_sample,
            st["p_sa"].astype(state_gdn.dtype), st["p_ca"].astype(state_gdn_conv.dtype),
            st["p_sb"].astype(state_ssd.dtype), st["p_cb"].astype(state_ssd_conv.dtype),
            st["s_sa"].astype(state_gdn.dtype), st["s_ca"].astype(state_gdn_conv.dtype),
            st["s_sb"].astype(state_ssd.dtype), st["s_cb"].astype(state_ssd_conv.dtype))
```
